```python
import numpy as np
import jax
import jax.numpy as jnp
from jax import lax

D_MODEL = 1024
BATCH = 8
SEQ = 2048
DEPTH = 2
DEC_BATCH = 128
DEC_SEQ = 8
PAST_LEN = 2048
PAGE_SIZE = 128

HEAD_DIM = 64
D_MIX = D_MODEL
H_SB = 6
H_DSA = 6
H_DSA_KV = 2
D_SB = H_SB * HEAD_DIM
D_DSA = H_DSA * HEAD_DIM
D_CONV = D_MIX - D_SB - D_DSA
D_DSA_KV = H_DSA_KV * HEAD_DIM
CONV_W = 3
N_IDX_HEADS = 8
IDX_DIM = 64
TOPK_MAX = 256
Q_BLOCK = 128
ROPE_THETA = 10000.0
LN_EPS = 1e-5
DN_ALPHA = (2 * DEPTH) ** 0.25
DN_BETA = (8 * DEPTH) ** -0.25

IN_SPLITS = (D_CONV, D_CONV, D_CONV, D_CONV,
             D_SB, D_SB, D_SB, D_SB,
             D_DSA, D_DSA_KV, D_DSA_KV, D_DSA,
             N_IDX_HEADS * IDX_DIM, IDX_DIM, N_IDX_HEADS)
D_IN = sum(IN_SPLITS)

kernel_name = 'hymba_conv_stickbreak_dsa_step'


def split_in(h):
    offs = np.cumsum(IN_SPLITS)[:-1].tolist()
    return jnp.split(h, offs, axis=-1)


def layer_norm(x, g, b):
    xf = x.astype(jnp.float32)
    mu = xf.mean(-1, keepdims=True)
    var = jnp.square(xf - mu).mean(-1, keepdims=True)
    y = (xf - mu) * lax.rsqrt(var + LN_EPS) * g.astype(jnp.float32) + b.astype(jnp.float32)
    return y.astype(x.dtype)


def rope(x, pos):
    half = x.shape[-1] // 2
    inv_freq = jnp.float32(ROPE_THETA) ** (-jnp.arange(half, dtype=jnp.float32) / half)
    ang = pos.astype(jnp.float32)[:, None] * inv_freq[None, :]
    cos = jnp.cos(ang)[:, None, :]
    sin = jnp.sin(ang)[:, None, :]
    xf = x.astype(jnp.float32)
    x1, x2 = xf[..., :half], xf[..., half:]
    return jnp.concatenate([x1 * cos - x2 * sin, x2 * cos + x1 * sin], axis=-1).astype(x.dtype)


def over_query_blocks(fn, q_arrays, q_pos):
    t = q_pos.shape[0]
    blk = Q_BLOCK if t % Q_BLOCK == 0 else t
    n = t // blk

    def split(a):
        return jnp.moveaxis(a.reshape(a.shape[0], n, blk, *a.shape[2:]), 1, 0)

    xs = tuple(split(a) for a in q_arrays) + (q_pos.reshape(n, blk),)
    out = lax.map(lambda a: fn(*a), xs)
    out = jnp.moveaxis(out, 0, 1)
    return out.reshape(out.shape[0], t, *out.shape[3:])


def stick_breaking(q, k, v, q_pos, k_pos):
    z = jnp.einsum('bqhd,bkhd->bhqk', q, k).astype(jnp.float32) * HEAD_DIM ** -0.5
    causal = (k_pos[None, :] < q_pos[:, None])[None, None]
    log_beta = jax.nn.log_sigmoid(z)
    log_keep = jnp.where(causal, jax.nn.log_sigmoid(-z), 0.0)
    after = lax.cumsum(log_keep, axis=3, reverse=True) - log_keep
    a = jnp.where(causal, jnp.exp(log_beta + after), 0.0)
    return jnp.einsum('bhqk,bkhd->bqhd', a.astype(v.dtype), v)


def dsa_attend(q, qi, wi, q_pos, k, v, ki, k_pos, topk):
    bq, tq = q.shape[0], q.shape[1]
    s = jnp.einsum('bqhe,bke->bqhk', qi, ki).astype(jnp.float32)
    score = jnp.einsum('bqhk,bqh->bqk', jax.nn.relu(s), wi.astype(jnp.float32))
    admissible = k_pos[None, :] <= q_pos[:, None]
    score = jnp.where(admissible[None], score, -jnp.inf)
    _, idx = lax.top_k(score, topk)
    take = jax.vmap(lambda rows, ids: rows[ids])
    kg = take(k, idx)
    vg = take(v, idx)
    valid = k_pos[idx] <= q_pos[None, :, None]
    qg = q.reshape(bq, tq, H_DSA_KV, H_DSA // H_DSA_KV, HEAD_DIM)
    logits = jnp.einsum('bqgrd,bqkgd->bqgrk', qg, kg).astype(jnp.float32) * HEAD_DIM ** -0.5
    logits = jnp.where(valid[:, :, None, None, :], logits, -jnp.inf)
    p = jax.nn.softmax(logits, axis=-1)
    o = jnp.einsum('bqgrk,bqkgd->bqgrd', p.astype(v.dtype), vg)
    return o.reshape(bq, tq, H_DSA, HEAD_DIM)


def trunk_layer(x, pos, past, conv_buf, w_in, conv_w, w_out, ln_g, ln_b):
    bn, t = x.shape[0], x.shape[1]
    h = jnp.einsum('btd,de->bte', x, w_in)
    (cb, cc, ch, cg, sq, sk, sv, sg, dq, dk, dv, dg, iq, ik, iw) = split_in(h)

    u = cc * ch
    u_full = jnp.concatenate([conv_buf.astype(u.dtype), u], axis=1)
    conv = conv_w[0] * u_full[:, 0:t]
    for i in range(1, CONV_W):
        conv = conv + conv_w[i] * u_full[:, i:i + t]
    y_a = cb * conv * jax.nn.silu(cg)
    new_conv = u_full[:, -(CONV_W - 1):]

    sb_new = jnp.stack([sk.reshape(bn, t, H_SB, HEAD_DIM),
                        sv.reshape(bn, t, H_SB, HEAD_DIM)], axis=2)
    dsa_new = jnp.stack([rope(dk.reshape(bn, t, H_DSA_KV, HEAD_DIM), pos),
                         dv.reshape(bn, t, H_DSA_KV, HEAD_DIM)], axis=2)
    ki_new = rope(ik[:, :, None, :], pos)[:, :, 0, :]

    if past is None:
        sb_all, dsa_all, ki_all, k_pos = sb_new, dsa_new, ki_new, pos
    else:
        p_sb, p_dsa, p_ki = past
        sb_all = jnp.concatenate([p_sb.astype(sb_new.dtype), sb_new], axis=1)
        dsa_all = jnp.concatenate([p_dsa.astype(dsa_new.dtype), dsa_new], axis=1)
        ki_all = jnp.concatenate([p_ki.astype(ki_new.dtype), ki_new], axis=1)
        k_pos = jnp.concatenate([jnp.arange(p_sb.shape[1], dtype=jnp.int32), pos])
    n_keys = sb_all.shape[1]

    q_sb = sq.reshape(bn, t, H_SB, HEAD_DIM)
    k_sb, v_sb = sb_all[:, :, 0], sb_all[:, :, 1]
    y_b = over_query_blocks(lambda qb, pb: stick_breaking(qb, k_sb, v_sb, pb, k_pos), (q_sb,), pos)
    y_b = y_b.reshape(bn, t, D_SB) * jax.nn.silu(sg)

    q_d = rope(dq.reshape(bn, t, H_DSA, HEAD_DIM), pos)
    qi = rope(iq.reshape(bn, t, N_IDX_HEADS, IDX_DIM), pos)
    wi = iw * (N_IDX_HEADS * IDX_DIM) ** -0.5
    k_d, v_d = dsa_all[:, :, 0], dsa_all[:, :, 1]
    topk = min(TOPK_MAX, n_keys // 4)
    y_c = over_query_blocks(
        lambda qb, qib, wib, pb: dsa_attend(qb, qib, wib, pb, k_d, v_d, ki_all, k_pos, topk),
        (q_d, qi, wi), pos)
    y_c = y_c.reshape(bn, t, D_DSA) * jax.nn.silu(dg)

    mix = jnp.concatenate([y_a, y_b, y_c], axis=-1)
    out = jnp.einsum('bte,ed->btd', mix, w_out)
    x_new = layer_norm(DN_ALPHA * x + out, ln_g, ln_b)
    return x_new, sb_new, dsa_new, ki_new, new_conv


def gather_pages(pool, page_table):
    g = pool[page_table]
    return g.reshape(g.shape[0], g.shape[1] * g.shape[2], *g.shape[3:])


def setup_inputs(seed: int = 0) -> dict:
    key = jax.random.key(seed)
    ks = jax.random.split(key, 12)
    n_pages = PAST_LEN // PAGE_SIZE
    n_used = DEC_BATCH * n_pages
    n_pool = n_used + n_used // 4
    f32 = jnp.float32
    x_prompt = jax.random.normal(ks[0], (BATCH, SEQ, D_MODEL), f32)
    x_sample = jax.random.normal(ks[1], (DEC_BATCH, DEC_SEQ, D_MODEL), f32)
    cache_sb_kv = jax.random.normal(ks[2], (DEPTH, n_pool, PAGE_SIZE, 2, H_SB, HEAD_DIM), f32)
    cache_dsa_kv = jax.random.normal(ks[3], (DEPTH, n_pool, PAGE_SIZE, 2, H_DSA_KV, HEAD_DIM), f32)
    cache_dsa_kidx = jax.random.normal(ks[4], (DEPTH, n_pool, PAGE_SIZE, IDX_DIM), f32)
    state_conv = jax.random.normal(ks[5], (DEPTH, DEC_BATCH, CONV_W - 1, D_CONV), f32)
    page_table = jax.random.permutation(ks[6], n_pool)[:n_used].reshape(DEC_BATCH, n_pages).astype(jnp.int32)
    w_in = jax.random.normal(ks[7], (DEPTH, D_MODEL, D_IN), f32) * D_MODEL ** -0.5
    conv_w = jax.random.normal(ks[8], (DEPTH, CONV_W, D_CONV), f32) * CONV_W ** -0.5
    w_out = jax.random.normal(ks[9], (DEPTH, D_MIX, D_MODEL), f32) * (D_MIX ** -0.5 * DN_BETA)
    ln_g = 1.0 + 0.02 * jax.random.normal(ks[10], (DEPTH, D_MODEL), f32)
    ln_b = 0.02 * jax.random.normal(ks[11], (DEPTH, D_MODEL), f32)
    return {'x_prompt': x_prompt, 'x_sample': x_sample,
            'cache_sb_kv': cache_sb_kv, 'cache_dsa_kv': cache_dsa_kv,
            'cache_dsa_kidx': cache_dsa_kidx, 'state_conv': state_conv,
            'page_table': page_table,
            'w_in': w_in, 'conv_w': conv_w, 'w_out': w_out, 'ln_g': ln_g, 'ln_b': ln_b}


def reference(x_prompt, x_sample, cache_sb_kv, cache_dsa_kv, cache_dsa_kidx, state_conv,
              page_table, w_in, conv_w, w_out, ln_g, ln_b):
    bp, tp = x_prompt.shape[0], x_prompt.shape[1]
    ts = x_sample.shape[1]
    past_len = page_table.shape[1] * PAGE_SIZE
    pos_p = jnp.arange(tp, dtype=jnp.int32)
    pos_s = past_len + jnp.arange(ts, dtype=jnp.int32)
    zero_buf = jnp.zeros((bp, CONV_W - 1, D_CONV), x_prompt.dtype)

    xp, xs = x_prompt, x_sample
    p_sb, p_dsa, p_ki, p_conv = [], [], [], []
    s_sb, s_dsa, s_ki, s_conv = [], [], [], []
    for l in range(DEPTH):
        xp, a, b, c, d = trunk_layer(xp, pos_p, None, zero_buf,
                                     w_in[l], conv_w[l], w_out[l], ln_g[l], ln_b[l])
        p_sb.append(a); p_dsa.append(b); p_ki.append(c); p_conv.append(d)
        past = (gather_pages(cache_sb_kv[l], page_table),
                gather_pages(cache_dsa_kv[l], page_table),
                gather_pages(cache_dsa_kidx[l], page_table))
        xs, a, b, c, d = trunk_layer(xs, pos_s, past, state_conv[l],
                                     w_in[l], conv_w[l], w_out[l], ln_g[l], ln_b[l])
        s_sb.append(a); s_dsa.append(b); s_ki.append(c); s_conv.append(d)

    return (xp, xs,
            jnp.stack(p_sb), jnp.stack(p_dsa), jnp.stack(p_ki), jnp.stack(p_conv),
            jnp.stack(s_sb), jnp.stack(s_dsa), jnp.stack(s_ki), jnp.stack(s_conv))
```

```python
import functools

import numpy as np
import jax
import jax.numpy as jnp
from jax import lax
from jax.experimental import pallas as pl
from jax.experimental.pallas import tpu as pltpu

F32 = jnp.float32
BF16 = jnp.bfloat16
I32 = jnp.int32

HEAD_DIM = 64
H_SB = 6
H_DSA = 6
H_DSA_KV = 2
D_SB = H_SB * HEAD_DIM
D_DSA = H_DSA * HEAD_DIM
D_CONV = 256
D_DSA_KV = H_DSA_KV * HEAD_DIM
CONV_W = 3
N_IDX_HEADS = 8
IDX_DIM = 64
TOPK_MAX = 256
ROPE_THETA = 10000.0
LN_EPS = 1e-5
PAGE_SIZE = 128
IN_SPLITS = (D_CONV, D_CONV, D_CONV, D_CONV, D_SB, D_SB, D_SB, D_SB,
             D_DSA, D_DSA_KV, D_DSA_KV, D_DSA, N_IDX_HEADS * IDX_DIM, IDX_DIM, N_IDX_HEADS)

LANES = 128
SUBLANES = 8
VMEM_LIMIT = 56 * 1024 * 1024

C_SB = 4 * D_CONV
C_DSA = C_SB + 4 * D_SB
C_IDX = C_DSA + 2 * D_DSA + 2 * D_DSA_KV
W_IDX = 768
NC = C_IDX + W_IDX
DSA_HEAD_ORDER = (0, 3, 1, 4, 2, 5)

NEG_BIG = -1e30
INT_MIN = -2 ** 31


def _dot(a, b):
    return jnp.dot(a, b, preferred_element_type=F32)


def _dot_nt(a, b):
    return lax.dot_general(a, b, (((1,), (1,)), ((), ())), preferred_element_type=F32)


def _silu(x):
    return x / (1.0 + jnp.exp(-x))


def _split_bf16(x):
    hi = x.astype(BF16)
    lo = (x - hi.astype(F32)).astype(BF16)
    return hi, lo


def _rope_chunk(x, cos, sin, first_half):
    up = pltpu.roll(x, LANES - 32, 1)
    dn = pltpu.roll(x, 32, 1)
    return x * cos + jnp.where(first_half, up, dn) * sin


def _softplus(z):
    return jnp.maximum(z, 0.0) + jnp.log1p(jnp.exp(-jnp.abs(z)))


def _to_key(s):
    b = pltpu.bitcast(s, I32)
    return jnp.where(b < 0, (b ^ jnp.int32(0x7FFFFFFF)) + 1, b)


def _inproj_kernel(*refs, sample, tiles_per_seq):
    if sample:
        x_ref, w_ref, cw_ref, cos_ref, sin_ref, st_ref = refs[:6]
        outs = refs[6:]
    else:
        x_ref, w_ref, cw_ref, cos_ref, sin_ref = refs[:5]
        outs = refs[5:-1]
        carry_ref = refs[-1]
    (ya_ref, u_ref, qsb_ref, sbkv_ref, sbkvb_ref, gsb_ref, qd_ref, dkv_ref, dkvb_ref,
     gd_ref, qi_ref, ki_ref, wi_ref) = outs

    xb = x_ref[...].astype(BF16)
    tm = xb.shape[0]
    cos = cos_ref[...]
    sin = sin_ref[...]
    lane = lax.broadcasted_iota(I32, (1, LANES), 1)
    first_half = (lane & (HEAD_DIM - 1)) < (HEAD_DIM // 2)

    def seg(a, b):
        return _dot(xb, w_ref[:, a:b])

    hc = seg(0, C_SB)
    cb, cc, ch, cg = (hc[:, i * D_CONV:(i + 1) * D_CONV] for i in range(4))
    u = cc * ch
    u_ref[...] = u
    cw = cw_ref[...]
    if sample:
        g = tm // SUBLANES
        u3 = u.reshape(g, SUBLANES, D_CONV)
        st = st_ref[...]
        r = lax.broadcasted_iota(I32, (1, SUBLANES, 1), 1)
        um1 = jnp.where(r == 0, st[:, 7:8, :], pltpu.roll(u3, 1, 1))
        um2 = jnp.where(r == 0, st[:, 6:7, :],
                        jnp.where(r == 1, st[:, 7:8, :], pltpu.roll(u3, 2, 1)))
        conv = cw[0:1, :] * um2 + cw[1:2, :] * um1 + cw[2:3, :] * u3
        conv = conv.reshape(tm, D_CONV)
    else:
        @pl.when(pl.program_id(0) % tiles_per_seq == 0)
        def _():
            carry_ref[...] = jnp.zeros_like(carry_ref)
        c = carry_ref[...]
        row = lax.broadcasted_iota(I32, (tm, 1), 0)
        um1 = jnp.where(row == 0, c[7:8, :], pltpu.roll(u, 1, 0))
        um2 = jnp.where(row == 0, c[6:7, :],
                        jnp.where(row == 1, c[7:8, :], pltpu.roll(u, 2, 0)))
        carry_ref[...] = u[tm - SUBLANES:, :]
        conv = cw[0:1, :] * um2 + cw[1:2, :] * um1 + cw[2:3, :] * u
    ya_ref[...] = cb * conv * _silu(cg)

    hs = seg(C_SB, C_DSA)
    qsb_ref[...] = (hs[:, :D_SB] * (HEAD_DIM ** -0.5)).astype(BF16)
    kv = hs[:, D_SB:3 * D_SB]
    sbkv_ref[...] = kv
    sbkvb_ref[...] = kv.astype(BF16)
    gsb_ref[...] = _silu(hs[:, 3 * D_SB:])

    hd = seg(C_DSA, C_IDX)
    for p in range(D_DSA // LANES):
        qc = _rope_chunk(hd[:, p * LANES:(p + 1) * LANES], cos, sin, first_half)
        qd_ref[:, p * LANES:(p + 1) * LANES] = (qc * (HEAD_DIM ** -0.5)).astype(BF16)
    dk = _rope_chunk(hd[:, D_DSA:D_DSA + D_DSA_KV], cos, sin, first_half)
    dv = hd[:, D_DSA + D_DSA_KV:D_DSA + 2 * D_DSA_KV]
    dkv_ref[:, :D_DSA_KV] = dk
    dkv_ref[:, D_DSA_KV:] = dv
    dkvb_ref[:, :D_DSA_KV] = dk.astype(BF16)
    dkvb_ref[:, D_DSA_KV:] = dv.astype(BF16)
    gd_ref[...] = _silu(hd[:, D_DSA + 2 * D_DSA_KV:])

    hi = seg(C_IDX, NC)
    nq = N_IDX_HEADS * IDX_DIM
    for p in range(nq // LANES):
        qi_ref[:, p * LANES:(p + 1) * LANES] = _rope_chunk(
            hi[:, p * LANES:(p + 1) * LANES], cos, sin, first_half)
    ki_ref[...] = _rope_chunk(hi[:, nq:nq + LANES], cos, sin, first_half)
    wi_ref[...] = hi[:, nq + LANES:nq + 2 * LANES] * (nq ** -0.5)


def _inproj(x, w_all, conv_w, layer, cos, sin, state, *, tm, tiles_per_seq, name):
    n_tok, d = x.shape
    sample = state is not None
    n_tiles = n_tok // tm
    widths = [(D_CONV, F32), (D_CONV, F32), (D_SB, BF16), (2 * D_SB, F32), (2 * D_SB, BF16),
              (D_SB, F32), (D_DSA, BF16), (2 * D_DSA_KV, F32), (2 * D_DSA_KV, BF16),
              (D_DSA, F32), (N_IDX_HEADS * IDX_DIM, F32), (LANES, F32), (LANES, F32)]
    out_shape = [jax.ShapeDtypeStruct((n_tok, w), dt) for w, dt in widths]
    out_specs = [pl.BlockSpec((tm, w), lambda i: (i, 0)) for w, _ in widths]
    in_specs = [
        pl.BlockSpec((tm, d), lambda i: (i, 0)),
        pl.BlockSpec((None, d, NC), lambda i: (layer, 0, 0)),
        pl.BlockSpec((None, CONV_W, D_CONV), lambda i: (layer, 0, 0)),
    ]
    if sample:
        in_specs += [pl.BlockSpec((tm, LANES), lambda i: (0, 0)),
                     pl.BlockSpec((tm, LANES), lambda i: (0, 0)),
                     pl.BlockSpec((tm // SUBLANES, SUBLANES, D_CONV), lambda i: (i, 0, 0))]
        args = (x, w_all, conv_w, cos, sin, state)
        scratch = []
    else:
        in_specs += [pl.BlockSpec((tm, LANES), lambda i: (i % tiles_per_seq, 0)),
                     pl.BlockSpec((tm, LANES), lambda i: (i % tiles_per_seq, 0))]
        args = (x, w_all, conv_w, cos, sin)
        scratch = [pltpu.VMEM((SUBLANES, D_CONV), F32)]
    return pl.pallas_call(
        functools.partial(_inproj_kernel, sample=sample, tiles_per_seq=tiles_per_seq),
        grid=(n_tiles,),
        in_specs=in_specs,
        out_specs=out_specs,
        out_shape=out_shape,
        scratch_shapes=scratch,
        compiler_params=pltpu.CompilerParams(
            dimension_semantics=("arbitrary",), vmem_limit_bytes=VMEM_LIMIT),
        name=name,
    )(*args)


def _sb_block(z, v_dot, tri, acc, carry, causal):
    sp = _softplus(z)
    lk = -sp
    lb = z - sp
    if causal is not None:
        lk = jnp.where(causal, lk, 0.0)
    hi, lo = _split_bf16(lk)
    after = _dot(hi, tri) + _dot(lo, tri) + carry
    a = jnp.exp(lb + after)
    if causal is not None:
        a = jnp.where(causal, a, 0.0)
    acc = acc + v_dot(a.astype(BF16))
    carry = after[:, 0:1] + lk[:, 0:1]
    return acc, carry


def _strict_lower(n):
    r = lax.broadcasted_iota(I32, (n, n), 0)
    c = lax.broadcasted_iota(I32, (n, n), 1)
    return (r > c).astype(BF16)


def _sb_prompt_kernel(q_ref, k_ref, v_ref, o_ref, *, tq):
    i = pl.program_id(2)
    q = q_ref[0]
    lane = lax.broadcasted_iota(I32, (1, LANES), 1)
    m0 = lane < HEAD_DIM
    zero = jnp.zeros_like(q)
    q2 = jnp.concatenate([jnp.where(m0, q, zero), jnp.where(m0, zero, q)], axis=0)
    tri = _strict_lower(tq)
    r = lax.broadcasted_iota(I32, (2 * tq, tq), 0)
    c = lax.broadcasted_iota(I32, (2 * tq, tq), 1)
    causal = c < jnp.where(r >= tq, r - tq, r)

    def block(j, acc, carry, mask):
        off = pl.multiple_of(j * tq, tq)
        k = k_ref[0, pl.ds(off, tq), :]
        v = v_ref[0, pl.ds(off, tq), :]
        z = _dot_nt(q2, k)
        return _sb_block(z, lambda a: _dot(a, v), tri, acc, carry, mask)

    acc = jnp.zeros((2 * tq, LANES), F32)
    carry = jnp.zeros((2 * tq, 1), F32)
    acc, carry = block(i, acc, carry, causal)
    acc, carry = lax.fori_loop(
        0, i, lambda jj, s: block(i - 1 - jj, s[0], s[1], None), (acc, carry))
    o_ref[0] = jnp.where(m0, acc[:tq], acc[tq:])


def _sb_prompt(q, kvb, *, tq):
    b, t, _ = q.shape
    n_pairs = D_SB // LANES
    return pl.pallas_call(
        functools.partial(_sb_prompt_kernel, tq=tq),
        grid=(b, n_pairs, t // tq),
        in_specs=[
            pl.BlockSpec((1, tq, LANES), lambda bi, p, i: (bi, i, p)),
            pl.BlockSpec((1, t, LANES), lambda bi, p, i: (bi, 0, p)),
            pl.BlockSpec((1, t, LANES), lambda bi, p, i: (bi, 0, n_pairs + p)),
        ],
        out_specs=pl.BlockSpec((1, tq, LANES), lambda bi, p, i: (bi, i, p)),
        out_shape=jax.ShapeDtypeStruct((b, t, D_SB), F32),
        compiler_params=pltpu.CompilerParams(
            dimension_semantics=("arbitrary", "arbitrary", "arbitrary"),
            vmem_limit_bytes=VMEM_LIMIT),
        name="sb_prompt",
    )(q, kvb, kvb)


def _sb_sample_kernel(pt_ref, q_ref, new_ref, *rest, n_pages):
    del pt_ref
    page_refs = rest[:n_pages]
    o_ref = rest[n_pages]
    ts = SUBLANES
    q = q_ref[0].astype(F32)
    lane = lax.broadcasted_iota(I32, (1, D_SB), 1)
    qbd = jnp.concatenate(
        [jnp.where((lane >= h * HEAD_DIM) & (lane < (h + 1) * HEAD_DIM), q, 0.0)
         for h in range(H_SB)], axis=0).astype(BF16)
    m = H_SB * ts
    acc = jnp.zeros((m, D_SB), F32)
    carry = jnp.zeros((m, 1), F32)

    new = new_ref[0]
    pad = jnp.zeros((PAGE_SIZE - ts, D_SB), F32)
    kn = jnp.concatenate([new[:, :D_SB], pad], axis=0).astype(BF16)
    vn = jnp.concatenate([new[:, D_SB:], pad], axis=0).astype(BF16)
    r = lax.broadcasted_iota(I32, (m, PAGE_SIZE), 0)
    c = lax.broadcasted_iota(I32, (m, PAGE_SIZE), 1)
    causal = c < (r & (ts - 1))
    acc, carry = _sb_block(_dot_nt(qbd, kn), lambda a: _dot(a, vn), _strict_lower(PAGE_SIZE),
                           acc, carry, causal)

    tri = _strict_lower(2 * PAGE_SIZE)
    for blk in reversed(range(n_pages // 2)):
        pa = page_refs[2 * blk]
        pb = page_refs[2 * blk + 1]
        kt = jnp.concatenate([pa[0], pb[0]], axis=1).astype(BF16)
        vt = jnp.concatenate([pa[1], pb[1]], axis=1).astype(BF16)
        acc, carry = _sb_block(_dot(qbd, kt), lambda a, vt=vt: _dot_nt(a, vt), tri,
                               acc, carry, None)

    out = jnp.zeros((ts, D_SB), F32)
    for h in range(H_SB):
        sel = (lane >= h * HEAD_DIM) & (lane < (h + 1) * HEAD_DIM)
        out = out + jnp.where(sel, acc[h * ts:(h + 1) * ts, :], 0.0)
    o_ref[0] = out


def _sb_sample(q, new_kv, cache_t, page_table, layer):
    nb, ts, _ = q.shape
    n_pages = page_table.shape[1]
    page_specs = [
        pl.BlockSpec((None, None, 2, D_SB, PAGE_SIZE),
                     lambda bi, pt, j=j: (layer, pt[bi, j], 0, 0, 0))
        for j in range(n_pages)]
    grid_spec = pltpu.PrefetchScalarGridSpec(
        num_scalar_prefetch=1,
        grid=(nb,),
        in_specs=[pl.BlockSpec((1, ts, D_SB), lambda bi, pt: (bi, 0, 0)),
                  pl.BlockSpec((1, ts, 2 * D_SB), lambda bi, pt: (bi, 0, 0))] + page_specs,
        out_specs=pl.BlockSpec((1, ts, D_SB), lambda bi, pt: (bi, 0, 0)),
    )
    return pl.pallas_call(
        functools.partial(_sb_sample_kernel, n_pages=n_pages),
        grid_spec=grid_spec,
        out_shape=jax.ShapeDtypeStruct((nb, ts, D_SB), F32),
        compiler_params=pltpu.CompilerParams(
            dimension_semantics=("arbitrary",), vmem_limit_bytes=VMEM_LIMIT),
        name="sb_sample",
    )(page_table, q, new_kv, *([cache_t] * n_pages))


def _kth_largest_key(count_ge, shape, topk):
    t0 = jnp.full(shape, INT_MIN, I32)
    zero = jnp.zeros(shape, I32)
    t = jnp.where(count_ge(zero) >= topk, zero, t0)

    def body(b, t):
        cand = t | (jnp.int32(1) << (30 - b))
        return jnp.where(count_ge(cand) >= topk, cand, t)

    return lax.fori_loop(0, 31, body, t)


def _incl_upper(n):
    r = lax.broadcasted_iota(I32, (n, n), 0)
    c = lax.broadcasted_iota(I32, (n, n), 1)
    return (r <= c).astype(BF16)


def _dsa_prompt_kernel(qi_ref, wi_ref, ki_ref, qd_ref, kv_ref, o_ref, key_scr, qs_scr,
                       *, tq, topk):
    i = pl.program_id(1)
    nblk = i + 1
    lane = lax.broadcasted_iota(I32, (1, LANES), 1)
    m0 = lane < HEAD_DIM
    wi = wi_ref[0]
    wcol = [wi[:, h:h + 1] for h in range(N_IDX_HEADS)]

    for h in range(N_IDX_HEADS):
        p, a = divmod(h, 2)
        qc = qi_ref[0, :, p * LANES:(p + 1) * LANES]
        qm = jnp.where(m0 if a == 0 else jnp.logical_not(m0), qc, 0.0)
        qh, ql = _split_bf16(qm)
        qs_scr[h] = jnp.concatenate([qh, ql, qh], axis=1)

    r = lax.broadcasted_iota(I32, (tq, tq), 0)
    c = lax.broadcasted_iota(I32, (tq, tq), 1)

    def score_block(j, diag):
        off = pl.multiple_of(j * tq, tq)
        kh, kl = _split_bf16(ki_ref[0, pl.ds(off, tq), :])
        kcat = jnp.concatenate([kh, kh, kl], axis=1)
        sc = jnp.zeros((tq, tq), F32)
        for h in range(N_IDX_HEADS):
            s = _dot_nt(qs_scr[h], kcat)
            sc = sc + jnp.maximum(s, 0.0) * wcol[h]
        key = _to_key(sc)
        if diag:
            key = jnp.where(c <= r, key, INT_MIN)
        key_scr[j] = key

    score_block(i, True)

    def score_body(j, carry):
        score_block(j, False)
        return carry

    lax.fori_loop(0, i, score_body, 0)

    def count(pred):
        def body(j, acc):
            hit = jnp.where(pred(key_scr[j]), 1.0, 0.0)
            for cc in range(tq // LANES):
                acc = acc + hit[:, cc * LANES:(cc + 1) * LANES]
            return acc
        acc = lax.fori_loop(0, nblk, body, jnp.zeros((tq, LANES), F32))
        return jnp.sum(acc, axis=1, keepdims=True)

    def count_ge(cand):
        cb = jnp.broadcast_to(cand, (tq, tq))
        return count(lambda k: k >= cb)

    thr = _kth_largest_key(count_ge, (tq, 1), float(topk))
    thr_b = jnp.broadcast_to(thr, (tq, tq))
    need = float(topk) - count(lambda k: k > thr_b)
    upper = _incl_upper(tq)

    def sel_body(j, tie_carry):
        key = key_scr[j]
        tie = key == thr_b
        tie_f = jnp.where(tie, 1.0, 0.0)
        incl = _dot(tie_f.astype(BF16), upper)
        before = incl - tie_f + tie_carry
        sel = ((key > thr_b) | (tie & (before < need))) & (key != INT_MIN)
        key_scr[j] = sel.astype(I32)
        return tie_carry + incl[:, tq - 1:tq]

    lax.fori_loop(0, nblk, sel_body, jnp.zeros((tq, 1), F32))

    for p in range(D_DSA // LANES):
        q = qd_ref[0, :, p * LANES:(p + 1) * LANES]
        zero = jnp.zeros_like(q)
        q2 = jnp.concatenate([jnp.where(m0, q, zero), jnp.where(m0, zero, q)], axis=0)

        def att_body(j, state, q2=q2):
            m_run, l_run, acc = state
            off = pl.multiple_of(j * tq, tq)
            kvb = kv_ref[0, pl.ds(off, tq), :]
            sel = key_scr[j] != 0
            sel2 = jnp.concatenate([sel, sel], axis=0)
            logit = jnp.where(sel2, _dot_nt(q2, kvb[:, :D_DSA_KV]), NEG_BIG)
            m_new = jnp.maximum(m_run, jnp.max(logit, axis=1, keepdims=True))
            alpha = jnp.exp(m_run - m_new)
            pr = jnp.where(sel2, jnp.exp(logit - m_new), 0.0)
            l_new = alpha * l_run + jnp.sum(pr, axis=1, keepdims=True)
            acc = alpha * acc + _dot(pr.astype(BF16), kvb[:, D_DSA_KV:])
            return m_new, l_new, acc

        init = (jnp.full((2 * tq, 1), NEG_BIG, F32), jnp.zeros((2 * tq, 1), F32),
                jnp.zeros((2 * tq, LANES), F32))
        _, l_run, acc = lax.fori_loop(0, nblk, att_body, init)
        o = acc / l_run
        o_ref[0, :, p * LANES:(p + 1) * LANES] = jnp.where(m0, o[:tq], o[tq:])


def _dsa_prompt(qi, wi, ki, qd, kvb, *, tq, topk):
    b, t, _ = qi.shape
    nq = N_IDX_HEADS * IDX_DIM
    return pl.pallas_call(
        functools.partial(_dsa_prompt_kernel, tq=tq, topk=topk),
        grid=(b, t // tq),
        in_specs=[
            pl.BlockSpec((1, tq, nq), lambda bi, i: (bi, i, 0)),
            pl.BlockSpec((1, tq, LANES), lambda bi, i: (bi, i, 0)),
            pl.BlockSpec((1, t, LANES), lambda bi, i: (bi, 0, 0)),
            pl.BlockSpec((1, tq, D_DSA), lambda bi, i: (bi, i, 0)),
            pl.BlockSpec((1, t, 2 * D_DSA_KV), lambda bi, i: (bi, 0, 0)),
        ],
        out_specs=pl.BlockSpec((1, tq, D_DSA), lambda bi, i: (bi, i, 0)),
        out_shape=jax.ShapeDtypeStruct((b, t, D_DSA), F32),
        scratch_shapes=[pltpu.VMEM((t // tq, tq, tq), I32),
                        pltpu.VMEM((N_IDX_HEADS, tq, 3 * LANES), BF16)],
        compiler_params=pltpu.CompilerParams(
            dimension_semantics=("arbitrary", "arbitrary"), vmem_limit_bytes=VMEM_LIMIT),
        name="dsa_prompt",
    )(qi, wi, ki, qd, kvb)


def _dsa_sample_kernel(pt_ref, qi_ref, wi_ref, kin_ref, qd_ref, kvn_ref, *rest, n_pages, topk):
    del pt_ref
    ki_pages = rest[:n_pages]
    kv_pages = rest[n_pages:2 * n_pages]
    o_ref = rest[2 * n_pages]
    ts = SUBLANES
    nblk = n_pages + 1
    lane = lax.broadcasted_iota(I32, (1, LANES), 1)
    m0 = lane < HEAD_DIM

    qi = qi_ref[0]
    qst = jnp.concatenate([qi[:, h * IDX_DIM:(h + 1) * IDX_DIM] for h in range(N_IDX_HEADS)],
                          axis=0)
    qh, ql = _split_bf16(qst)
    wi = wi_ref[0]
    wcol = [wi[:, h:h + 1] for h in range(N_IDX_HEADS)]

    def head_sum(s):
        sc = jnp.zeros((ts, PAGE_SIZE), F32)
        for h in range(N_IDX_HEADS):
            sc = sc + jnp.maximum(s[h * ts:(h + 1) * ts, :], 0.0) * wcol[h]
        return sc

    keys = []
    for j in range(n_pages):
        kh, kl = _split_bf16(ki_pages[j][...])
        s = _dot(qh, kh) + _dot(ql, kh) + _dot(qh, kl)
        keys.append(_to_key(head_sum(s)))
    pad64 = jnp.zeros((PAGE_SIZE - ts, IDX_DIM), F32)
    knh, knl = _split_bf16(jnp.concatenate([kin_ref[0][:, :IDX_DIM], pad64], axis=0))
    s = _dot_nt(qh, knh) + _dot_nt(ql, knh) + _dot_nt(qh, knl)
    r = lax.broadcasted_iota(I32, (ts, PAGE_SIZE), 0)
    c = lax.broadcasted_iota(I32, (ts, PAGE_SIZE), 1)
    keys.append(jnp.where(c <= r, _to_key(head_sum(s)), INT_MIN))
    key = jnp.concatenate(keys, axis=1)

    def count_ge(cand):
        return jnp.sum(jnp.where(key >= cand, 1.0, 0.0), axis=1, keepdims=True)

    thr = _kth_largest_key(count_ge, (ts, 1), float(topk))
    need = float(topk) - jnp.sum(jnp.where(key > thr, 1.0, 0.0), axis=1, keepdims=True)
    upper = _incl_upper(PAGE_SIZE)
    tie_carry = jnp.zeros((ts, 1), F32)
    sels = []
    for j in range(nblk):
        kj = key[:, j * PAGE_SIZE:(j + 1) * PAGE_SIZE]
        tie = kj == thr
        tie_f = jnp.where(tie, 1.0, 0.0)
        incl = _dot(tie_f.astype(BF16), upper)
        before = incl - tie_f + tie_carry
        sels.append(((kj > thr) | (tie & (before < need))) & (kj != INT_MIN))
        tie_carry = tie_carry + incl[:, PAGE_SIZE - 1:PAGE_SIZE]

    qd = qd_ref[0].astype(F32)
    rows = []
    for p in range(D_DSA // LANES):
        qp = qd[:, p * LANES:(p + 1) * LANES]
        rows += [jnp.where(m0, qp, 0.0), jnp.where(m0, 0.0, qp)]
    qbd = jnp.concatenate(rows, axis=0).astype(BF16)
    n_rep = len(rows)

    kvn = kvn_ref[0]
    pad128 = jnp.zeros((PAGE_SIZE - ts, D_DSA_KV), F32)
    kdn = jnp.concatenate([kvn[:, :D_DSA_KV], pad128], axis=0).astype(BF16)
    vdn = jnp.concatenate([kvn[:, D_DSA_KV:], pad128], axis=0).astype(BF16)
    logits = [_dot(qbd, kv_pages[j][0].astype(BF16)) for j in range(n_pages)]
    logits.append(_dot_nt(qbd, kdn))
    sel_all = [jnp.concatenate([sels[j]] * n_rep, axis=0) for j in range(nblk)]
    logits = [jnp.where(sel_all[j], logits[j], NEG_BIG) for j in range(nblk)]
    m_row = logits[0].max(axis=1, keepdims=True)
    for j in range(1, nblk):
        m_row = jnp.maximum(m_row, logits[j].max(axis=1, keepdims=True))
    l_row = jnp.zeros((n_rep * ts, 1), F32)
    acc = jnp.zeros((n_rep * ts, D_DSA_KV), F32)
    for j in range(nblk):
        pr = jnp.where(sel_all[j], jnp.exp(logits[j] - m_row), 0.0)
        l_row = l_row + jnp.sum(pr, axis=1, keepdims=True)
        if j < n_pages:
            acc = acc + _dot_nt(pr.astype(BF16), kv_pages[j][1].astype(BF16))
        else:
            acc = acc + _dot(pr.astype(BF16), vdn)
    o = acc / l_row
    for p in range(D_DSA // LANES):
        o_ref[0, :, p * LANES:(p + 1) * LANES] = jnp.where(
            m0, o[(2 * p) * ts:(2 * p + 1) * ts], o[(2 * p + 1) * ts:(2 * p + 2) * ts])


def _dsa_sample(qi, wi, ki_new, qd, kv_new, cache_ki_t, cache_kv_t, page_table, layer, *, topk):
    nb, ts, _ = qi.shape
    n_pages = page_table.shape[1]
    nq = N_IDX_HEADS * IDX_DIM
    ki_specs = [
        pl.BlockSpec((None, None, IDX_DIM, PAGE_SIZE),
                     lambda bi, pt, j=j: (layer, pt[bi, j], 0, 0))
        for j in range(n_pages)]
    kv_specs = [
        pl.BlockSpec((None, None, 2, D_DSA_KV, PAGE_SIZE),
                     lambda bi, pt, j=j: (layer, pt[bi, j], 0, 0, 0))
        for j in range(n_pages)]
    grid_spec = pltpu.PrefetchScalarGridSpec(
        num_scalar_prefetch=1,
        grid=(nb,),
        in_specs=[pl.BlockSpec((1, ts, nq), lambda bi, pt: (bi, 0, 0)),
                  pl.BlockSpec((1, ts, LANES), lambda bi, pt: (bi, 0, 0)),
                  pl.BlockSpec((1, ts, LANES), lambda bi, pt: (bi, 0, 0)),
                  pl.BlockSpec((1, ts, D_DSA), lambda bi, pt: (bi, 0, 0)),
                  pl.BlockSpec((1, ts, 2 * D_DSA_KV), lambda bi, pt: (bi, 0, 0))]
        + ki_specs + kv_specs,
        out_specs=pl.BlockSpec((1, ts, D_DSA), lambda bi, pt: (bi, 0, 0)),
    )
    return pl.pallas_call(
        functools.partial(_dsa_sample_kernel, n_pages=n_pages, topk=topk),
        grid_spec=grid_spec,
        out_shape=jax.ShapeDtypeStruct((nb, ts, D_DSA), F32),
        compiler_params=pltpu.CompilerParams(
            dimension_semantics=("arbitrary",), vmem_limit_bytes=VMEM_LIMIT),
        name="dsa_sample",
    )(page_table, qi, wi, ki_new, qd, kv_new, *([cache_ki_t] * n_pages), *([cache_kv_t] * n_pages))


def _outproj_kernel(ya_ref, yb_ref, gsb_ref, yc_ref, gd_ref, x_ref, w_ref, g_ref, b_ref, o_ref,
                    *, alpha):
    out = _dot(ya_ref[...].astype(BF16), w_ref[:D_CONV, :])
    out = out + _dot((yb_ref[...] * gsb_ref[...]).astype(BF16), w_ref[D_CONV:D_CONV + D_SB, :])
    out = out + _dot((yc_ref[...] * gd_ref[...]).astype(BF16), w_ref[D_CONV + D_SB:, :])
    v = alpha * x_ref[...] + out
    mu = jnp.mean(v, axis=-1, keepdims=True)
    d = v - mu
    var = jnp.mean(d * d, axis=-1, keepdims=True)
    o_ref[...] = d * lax.rsqrt(var + LN_EPS) * g_ref[...] + b_ref[...]


def _outproj(ya, yb, gsb, yc, gd, x, w_out, ln_g, ln_b, layer, *, tm, alpha, name):
    n_tok, d = x.shape
    row = lambda w: pl.BlockSpec((tm, w), lambda i: (i, 0))
    return pl.pallas_call(
        functools.partial(_outproj_kernel, alpha=alpha),
        grid=(n_tok // tm,),
        in_specs=[row(D_CONV), row(D_SB), row(D_SB), row(D_DSA), row(D_DSA), row(d),
                  pl.BlockSpec((None, d, d), lambda i: (layer, 0, 0)),
                  pl.BlockSpec((None, 1, d), lambda i: (layer, 0, 0)),
                  pl.BlockSpec((None, 1, d), lambda i: (layer, 0, 0))],
        out_specs=row(d),
        out_shape=jax.ShapeDtypeStruct((n_tok, d), F32),
        compiler_params=pltpu.CompilerParams(
            dimension_semantics=("arbitrary",), vmem_limit_bytes=VMEM_LIMIT),
        name=name,
    )(ya, yb, gsb, yc, gd, x, w_out, ln_g, ln_b)


def _rope_tables(pos):
    half = HEAD_DIM // 2
    inv_freq = jnp.float32(ROPE_THETA) ** (-jnp.arange(half, dtype=F32) / half)
    ang = pos.astype(F32)[:, None] * inv_freq[None, :]
    cos = jnp.cos(ang)
    sin = jnp.sin(ang)
    return (jnp.concatenate([cos, cos, cos, cos], axis=1),
            jnp.concatenate([-sin, sin, -sin, sin], axis=1))


def _prep_w_in(w_in):
    depth, d, _ = w_in.shape
    o = np.cumsum((0,) + IN_SPLITS)
    (cb, cc, ch, cg, sq, sk, sv, sg, dq, dk, dv, dg, iq, ik, iw) = (
        w_in[:, :, o[i]:o[i + 1]] for i in range(len(IN_SPLITS)))
    order = np.asarray(DSA_HEAD_ORDER)

    def regroup(w):
        return w.reshape(depth, d, H_DSA, HEAD_DIM)[:, :, order].reshape(depth, d, D_DSA)

    pad = jnp.zeros((depth, d, W_IDX - N_IDX_HEADS * IDX_DIM - 2 * IDX_DIM - N_IDX_HEADS), w_in.dtype)
    w = jnp.concatenate([cb, cc, ch, cg, sq, sk, sv, sg, regroup(dq), dk, dv, regroup(dg),
                         iq, ik, ik, iw, pad], axis=-1)
    return w.astype(BF16)


def _prep_w_out(w_out):
    depth, _, d = w_out.shape
    order = np.asarray(DSA_HEAD_ORDER)
    wc = w_out[:, D_CONV + D_SB:, :].reshape(depth, H_DSA, HEAD_DIM, d)[:, order]
    return jnp.concatenate([w_out[:, :D_CONV + D_SB, :], wc.reshape(depth, D_DSA, d)],
                           axis=1).astype(BF16)


def _pick_tile(n, target):
    t = min(n, target)
    while n % t:
        t //= 2
    return t


def kernel(x_prompt, x_sample, cache_sb_kv, cache_dsa_kv, cache_dsa_kidx, state_conv, page_table,
           w_in, conv_w, w_out, ln_g, ln_b):
    bp, tp, d = x_prompt.shape
    nb, ts, _ = x_sample.shape
    depth = w_in.shape[0]
    n_pages = page_table.shape[1]
    n_pool = cache_sb_kv.shape[1]
    past_len = n_pages * PAGE_SIZE
    assert ts == SUBLANES and n_pages % 2 == 0 and cache_sb_kv.shape[2] == PAGE_SIZE
    alpha = (2 * depth) ** 0.25

    tq = _pick_tile(tp, 256)
    tm_p = _pick_tile(tp, 512)
    tm_s = _pick_tile(nb * ts, 512)
    topk_p = min(TOPK_MAX, tp // 4)
    topk_s = min(TOPK_MAX, (past_len + ts) // 4)

    w_all = _prep_w_in(w_in)
    w_o = _prep_w_out(w_out)
    g3 = ln_g.reshape(depth, 1, d)
    b3 = ln_b.reshape(depth, 1, d)
    cos_p, sin_p = _rope_tables(jnp.arange(tp, dtype=I32))
    cos_s, sin_s = _rope_tables(past_len + jnp.arange(ts, dtype=I32))
    cos_s = jnp.tile(cos_s, (tm_s // ts, 1))
    sin_s = jnp.tile(sin_s, (tm_s // ts, 1))
    state8 = jnp.pad(state_conv, ((0, 0), (0, 0), (SUBLANES - (CONV_W - 1), 0), (0, 0)))
    sb_t = jnp.transpose(cache_sb_kv, (0, 1, 3, 4, 5, 2)).reshape(depth, n_pool, 2, D_SB, PAGE_SIZE)
    dkv_t = jnp.transpose(cache_dsa_kv, (0, 1, 3, 4, 5, 2)).reshape(
        depth, n_pool, 2, D_DSA_KV, PAGE_SIZE)
    ki_t = jnp.transpose(cache_dsa_kidx, (0, 1, 3, 2))

    xp = x_prompt.reshape(bp * tp, d)
    xs = x_sample.reshape(nb * ts, d)
    leaves = [[] for _ in range(8)]
    for l in range(depth):
        (ya, u, qsb, sbkv, sbkvb, gsb, qd, dkv, dkvb, gd, qi, ki, wi) = _inproj(
            xp, w_all, conv_w, l, cos_p, sin_p, None, tm=tm_p, tiles_per_seq=tp // tm_p,
            name="inproj_prompt")
        yb = _sb_prompt(qsb.reshape(bp, tp, D_SB), sbkvb.reshape(bp, tp, 2 * D_SB), tq=tq)
        yc = _dsa_prompt(qi.reshape(bp, tp, -1), wi.reshape(bp, tp, LANES),
                         ki.reshape(bp, tp, LANES), qd.reshape(bp, tp, D_DSA),
                         dkvb.reshape(bp, tp, 2 * D_DSA_KV), tq=tq, topk=topk_p)
        xp = _outproj(ya, yb.reshape(bp * tp, D_SB), gsb, yc.reshape(bp * tp, D_DSA), gd, xp,
                      w_o, g3, b3, l, tm=tm_p, alpha=alpha, name="outproj_prompt")
        leaves[0].append(sbkv.reshape(bp, tp, 2, H_SB, HEAD_DIM))
        leaves[1].append(dkv.reshape(bp, tp, 2, H_DSA_KV, HEAD_DIM))
        leaves[2].append(ki[:, :IDX_DIM].reshape(bp, tp, IDX_DIM))
        leaves[3].append(u.reshape(bp, tp, D_CONV)[:, tp - (CONV_W - 1):, :])

        (ya, u, qsb, sbkv, sbkvb, gsb, qd, dkv, dkvb, gd, qi, ki, wi) = _inproj(
            xs, w_all, conv_w, l, cos_s, sin_s, state8[l], tm=tm_s, tiles_per_seq=1,
            name="inproj_sample")
        yb = _sb_sample(qsb.reshape(nb, ts, D_SB), sbkv.reshape(nb, ts, 2 * D_SB), sb_t,
                        page_table, l)
        yc = _dsa_sample(qi.reshape(nb, ts, -1), wi.reshape(nb, ts, LANES),
                         ki.reshape(nb, ts, LANES), qd.reshape(nb, ts, D_DSA),
                         dkv.reshape(nb, ts, 2 * D_DSA_KV), ki_t, dkv_t, page_table, l,
                         topk=topk_s)
        xs = _outproj(ya, yb.reshape(nb * ts, D_SB), gsb, yc.reshape(nb * ts, D_DSA), gd, xs,
                      w_o, g3, b3, l, tm=tm_s, alpha=alpha, name="outproj_sample")
        leaves[4].append(sbkv.reshape(nb, ts, 2, H_SB, HEAD_DIM))
        leaves[5].append(dkv.reshape(nb, ts, 2, H_DSA_KV, HEAD_DIM))
        leaves[6].append(ki[:, :IDX_DIM].reshape(nb, ts, IDX_DIM))
        leaves[7].append(u.reshape(nb, ts, D_CONV)[:, ts - (CONV_W - 1):, :])

    return (xp.reshape(bp, tp, d), xs.reshape(nb, ts, d)) + tuple(jnp.stack(x) for x in leaves)
```
